```python
import math
import jax, jax.numpy as jnp
from jax import lax
import numpy as np

D_MODEL = 1024
BATCH = 8
SEQ = 2048
DEPTH = 4
DEC_BATCH = 128
DEC_SEQ = 1
PAST_LEN = 8192
PAGE_SIZE = 128

N_MIXERS = 4
N_HEADS = 16
HEAD_DIM = 64
N_KV_HEADS = 4
GROUP = N_HEADS // N_KV_HEADS
Q_WIDTH = N_HEADS * HEAD_DIM
KV_WIDTH = N_KV_HEADS * HEAD_DIM
ATTN_SCALE = HEAD_DIM ** -0.5
N_BUCKETS = 32
MAX_DISTANCE = 128
IDX_HEADS = 8
IDX_DIM = 64
IDX_SCALE = (IDX_HEADS * IDX_DIM) ** -0.5
TOPK_MAX = 256
Q_BLOCK = 128
A_SPLITS = (Q_WIDTH, Q_WIDTH + KV_WIDTH, Q_WIDTH + 2 * KV_WIDTH,
            Q_WIDTH + 2 * KV_WIDTH + IDX_HEADS * IDX_DIM,
            Q_WIDTH + 2 * KV_WIDTH + IDX_HEADS * IDX_DIM + IDX_DIM)
A_IN_WIDTH = A_SPLITS[-1] + IDX_HEADS
WINDOW = 128
B_IN_WIDTH = Q_WIDTH + 2 * KV_WIDTH
CHUNK = 128
GMLP_INNER = 1024
GMLP_GROUPS = 8
GMLP_GROUP_DIM = GMLP_INNER // GMLP_GROUPS
CONV_WIDTH = 31
CONV_DIM = D_MODEL
N_EXPERTS = 128
TOP_K = 8
D_EXPERT = 256
ROUTED_SCALE = 2.5
EXPERT_BLOCK = 32
EPS = 1e-6

kernel_name = 'hybrid_dsa_swa_gmlp_conv_moe_step'


def rms_norm(x, g):
    xf = x.astype(jnp.float32)
    y = xf * lax.rsqrt(jnp.mean(xf * xf, axis=-1, keepdims=True) + EPS)
    return (y * g.astype(jnp.float32)).astype(x.dtype)


def layer_norm(x, g, b):
    xf = x.astype(jnp.float32)
    xc = xf - jnp.mean(xf, axis=-1, keepdims=True)
    y = xc * lax.rsqrt(jnp.mean(xc * xc, axis=-1, keepdims=True) + EPS)
    return (y * g.astype(jnp.float32) + b.astype(jnp.float32)).astype(x.dtype)


def adaln(x, c, norm_g, w, b):
    mod = (jax.nn.silu(c) @ w + b)[:, None, :]
    shift, scale, gate = jnp.split(mod, 3, axis=-1)
    return rms_norm(x, norm_g) * (1 + scale) + shift, gate


def t5_bucket(dist):
    max_exact = N_BUCKETS // 2
    d = jnp.maximum(dist, 0)
    log_ratio = jnp.log(jnp.maximum(d, 1).astype(jnp.float32) / max_exact) / math.log(MAX_DISTANCE / max_exact)
    large = jnp.minimum(max_exact + (log_ratio * (N_BUCKETS - max_exact)).astype(jnp.int32), N_BUCKETS - 1)
    return jnp.where(d < max_exact, d, large)


def head_bias(rel_bias, dist):
    return rel_bias[t5_bucket(dist)].astype(jnp.float32).reshape(dist.shape + (N_KV_HEADS, GROUP))


def gather_rows(a, idx):
    return jax.vmap(lambda ab, ib: ab[ib])(a, idx)


def a_project(h, w_in, q_g, k_g):
    n, t, _ = h.shape
    q, k, v, qi, ki, wi = jnp.split(h @ w_in, A_SPLITS, axis=-1)
    q = rms_norm(q.reshape(n, t, N_HEADS, HEAD_DIM), q_g)
    k = rms_norm(k.reshape(n, t, N_KV_HEADS, HEAD_DIM), k_g)
    v = v.reshape(n, t, N_KV_HEADS, HEAD_DIM)
    qi = qi.reshape(n, t, IDX_HEADS, IDX_DIM)
    return q, k, v, qi, ki, wi


def index_scores(qi, wi, ki):
    dots = jnp.einsum('bqhd,bsd->bqhs', qi, ki).astype(jnp.float32)
    return jnp.einsum('bqh,bqhs->bqs', wi.astype(jnp.float32), jax.nn.relu(dots)) * IDX_SCALE


def select_keys(scores, tpos, n_keys):
    spos = jnp.arange(scores.shape[-1])
    scores = jnp.where(spos[None, None, :] <= tpos[None, :, None], scores, -jnp.inf)
    _, sel = lax.top_k(scores, min(TOPK_MAX, n_keys // 4))
    return sel


def sparse_attend(q, tpos, ks, vs, sel, rel_bias):
    b, nq = q.shape[:2]
    qg = q.reshape(b, nq, N_KV_HEADS, GROUP, HEAD_DIM)
    logits = jnp.einsum('bqkgd,bqskd->bqkgs', qg, ks).astype(jnp.float32) * ATTN_SCALE
    dist = tpos[None, :, None] - sel
    logits = logits + head_bias(rel_bias, dist).transpose(0, 1, 3, 4, 2)
    logits = jnp.where((dist >= 0)[:, :, None, None, :], logits, -jnp.inf)
    probs = jax.nn.softmax(logits, axis=-1).astype(vs.dtype)
    out = jnp.einsum('bqkgs,bqskd->bqkgd', probs, vs)
    return out.reshape(b, nq, Q_WIDTH)


def mixer_a_prompt(h, w_in, q_g, k_g, w_out, rel_bias):
    n, t, _ = h.shape
    q, k, v, qi, ki, wi = a_project(h, w_in, q_g, k_g)
    nblk = t // Q_BLOCK

    def to_blocks(a):
        return jnp.moveaxis(a.reshape((n, nblk, Q_BLOCK) + a.shape[2:]), 1, 0)

    def block(args):
        qb, qib, wib, j = args
        tpos = j * Q_BLOCK + jnp.arange(Q_BLOCK)
        sel = select_keys(index_scores(qib, wib, ki), tpos, t)
        return sparse_attend(qb, tpos, gather_rows(k, sel), gather_rows(v, sel), sel, rel_bias)

    out = lax.map(block, (to_blocks(q), to_blocks(qi), to_blocks(wi), jnp.arange(nblk)))
    out = jnp.moveaxis(out, 0, 1).reshape(n, t, Q_WIDTH)
    return out @ w_out, k, v, ki


def mixer_a_sample(h, cache_k, cache_v, cache_kidx, page_table, w_in, q_g, k_g, w_out, rel_bias):
    db, ds, _ = h.shape
    past = page_table.shape[1] * PAGE_SIZE
    q, k, v, qi, ki, wi = a_project(h, w_in, q_g, k_g)
    ki_all = jnp.concatenate([cache_kidx[page_table].reshape(db, past, IDX_DIM), ki], axis=1)
    tpos = past + jnp.arange(ds)
    sel = select_keys(index_scores(qi, wi, ki_all), tpos, past + ds)
    sel_past = jnp.minimum(sel, past - 1)
    phys = jnp.take_along_axis(page_table, (sel_past // PAGE_SIZE).reshape(db, -1), axis=1).reshape(sel.shape)
    off = sel_past % PAGE_SIZE
    sel_new = jnp.clip(sel - past, 0, ds - 1)
    in_past = (sel < past)[..., None, None]
    ks = jnp.where(in_past, cache_k[phys, off], gather_rows(k, sel_new))
    vs = jnp.where(in_past, cache_v[phys, off], gather_rows(v, sel_new))
    out = sparse_attend(q, tpos, ks, vs, sel, rel_bias)
    return out @ w_out, k, v, ki


def b_project(h, w_in, q_g, k_g):
    n, t, _ = h.shape
    q, k, v = jnp.split(h @ w_in, (Q_WIDTH, Q_WIDTH + KV_WIDTH), axis=-1)
    q = rms_norm(q.reshape(n, t, N_HEADS, HEAD_DIM), q_g)
    k = rms_norm(k.reshape(n, t, N_KV_HEADS, HEAD_DIM), k_g)
    return q, k, v.reshape(n, t, N_KV_HEADS, HEAD_DIM)


def window_terms(qpos, kpos, rel_bias):
    dist = qpos[:, None] - kpos[None, :]
    mask = (dist >= 0) & (dist < WINDOW)
    bias = head_bias(rel_bias, dist).transpose(2, 3, 0, 1)
    return mask, bias


def sink_softmax(logits, sinks):
    sink_col = jnp.broadcast_to(sinks, logits.shape[:-1] + (1,))
    return jax.nn.softmax(jnp.concatenate([logits, sink_col], axis=-1), axis=-1)[..., :-1]


def mixer_b_prompt(h, w_in, q_g, k_g, sinks, w_out, rel_bias):
    n, t, _ = h.shape
    nb = t // WINDOW
    q, k, v = b_project(h, w_in, q_g, k_g)
    qb = q.reshape(n, nb, WINDOW, N_KV_HEADS, GROUP, HEAD_DIM)

    def with_prev(a):
        ab = a.reshape(n, nb, WINDOW, N_KV_HEADS, HEAD_DIM)
        prev = jnp.pad(ab, ((0, 0), (1, 0), (0, 0), (0, 0), (0, 0)))[:, :-1]
        return jnp.concatenate([prev, ab], axis=2)

    kk, vv = with_prev(k), with_prev(v)
    kpos = jnp.arange(2 * WINDOW)
    mask, bias = window_terms(WINDOW + jnp.arange(WINDOW), kpos, rel_bias)
    exists = (jnp.arange(nb)[:, None, None] > 0) | (kpos >= WINDOW)[None, None, :]
    mask = mask[None] & exists
    logits = jnp.einsum('bnqkgd,bnskd->bnkgqs', qb, kk).astype(jnp.float32) * ATTN_SCALE + bias
    logits = jnp.where(mask[None, :, None, None], logits, -jnp.inf)
    probs = sink_softmax(logits, sinks.astype(jnp.float32).reshape(N_KV_HEADS, GROUP, 1, 1))
    out = jnp.einsum('bnkgqs,bnskd->bnqkgd', probs.astype(vv.dtype), vv).reshape(n, t, Q_WIDTH)
    return out @ w_out, k[:, -WINDOW:], v[:, -WINDOW:]


def mixer_b_sample(h, buf_k, buf_v, w_in, q_g, k_g, sinks, w_out, rel_bias):
    db, ds, _ = h.shape
    q, k, v = b_project(h, w_in, q_g, k_g)
    kk = jnp.concatenate([buf_k, k], axis=1)
    vv = jnp.concatenate([buf_v, v], axis=1)
    mask, bias = window_terms(WINDOW + jnp.arange(ds), jnp.arange(WINDOW + ds), rel_bias)
    qg = q.reshape(db, ds, N_KV_HEADS, GROUP, HEAD_DIM)
    logits = jnp.einsum('bqkgd,bskd->bkgqs', qg, kk).astype(jnp.float32) * ATTN_SCALE + bias
    logits = jnp.where(mask, logits, -jnp.inf)
    probs = sink_softmax(logits, sinks.astype(jnp.float32).reshape(N_KV_HEADS, GROUP, 1, 1))
    out = jnp.einsum('bkgqs,bskd->bqkgd', probs.astype(vv.dtype), vv).reshape(db, ds, Q_WIDTH)
    return out @ w_out, kk[:, -WINDOW:], vv[:, -WINDOW:]


def mixer_c(h, w_in, ln_g, ln_b, w_s, b_s, w_out, rows):
    n, t, _ = h.shape
    u, v = jnp.split(jax.nn.gelu(h @ w_in, approximate=False), 2, axis=-1)
    v = layer_norm(v, ln_g, ln_b)
    vc = v.reshape(n, t // rows, rows, GMLP_GROUPS, GMLP_GROUP_DIM)
    ws = jnp.tril(w_s)[:, :rows, :rows]
    mixed = jnp.einsum('gts,bcsgd->bctgd', ws, vc) + b_s[:, :rows].T[:, :, None]
    y = u * mixed.reshape(n, t, GMLP_INNER)
    return y @ w_out, v


def mixer_d(h, hist, w_in, w_dw, b_dw, ln_g, ln_b, w_out):
    a, g = jnp.split(h @ w_in, 2, axis=-1)
    y = a * jax.nn.sigmoid(g)
    yh = jnp.concatenate([hist, y], axis=1)
    conv = lax.conv_general_dilated(yh, w_dw[:, None, :], window_strides=(1,), padding='VALID',
                                    dimension_numbers=('NWC', 'WIO', 'NWC'),
                                    feature_group_count=CONV_DIM) + b_dw
    z = jax.nn.silu(layer_norm(conv, ln_g, ln_b))
    return z @ w_out, yh[:, -(CONV_WIDTH - 1):]


def swiglu(x, w_gate, w_up, w_down):
    return (jax.nn.silu(x @ w_gate) * (x @ w_up)) @ w_down


def routed_experts(xt, idx, gates, w_gate, w_up, w_down):
    n_tok, d = xt.shape
    n_assign = n_tok * TOP_K
    flat_e = idx.reshape(-1)
    order = jnp.argsort(flat_e)
    e_sorted = flat_e[order]
    counts = jnp.bincount(flat_e, length=N_EXPERTS)
    padded = (counts + EXPERT_BLOCK - 1) // EXPERT_BLOCK * EXPERT_BLOCK
    start = jnp.cumsum(counts) - counts
    pend = jnp.cumsum(padded)
    pstart = pend - padded
    dest = pstart[e_sorted] + jnp.arange(n_assign) - start[e_sorted]
    n_blocks = (n_assign + N_EXPERTS * (EXPERT_BLOCK - 1)) // EXPERT_BLOCK + 1
    n_rows = n_blocks * EXPERT_BLOCK
    row_tok = jnp.full((n_rows,), n_tok, jnp.int32).at[dest].set((order // TOP_K).astype(jnp.int32))
    x_rows = jnp.concatenate([xt, jnp.zeros((1, d), xt.dtype)], axis=0)[row_tok]
    block_e = jnp.minimum(jnp.searchsorted(pend, jnp.arange(n_blocks) * EXPERT_BLOCK, side='right'),
                          N_EXPERTS - 1)

    def run_block(args):
        xb, e = args
        return swiglu(xb, w_gate[e], w_up[e], w_down[e])

    y_rows = lax.map(run_block, (x_rows.reshape(n_blocks, EXPERT_BLOCK, d), block_e)).reshape(n_rows, d)
    y_assign = jnp.zeros((n_assign, d), y_rows.dtype).at[order].set(y_rows[dest])
    return jnp.einsum('tk,tkd->td', gates.astype(y_rows.dtype), y_assign.reshape(n_tok, TOP_K, d))


def moe(h, router_w, router_bias, w_gate, w_up, w_down, s_gate, s_up, s_down):
    n, t, d = h.shape
    xt = h.reshape(n * t, d)
    scores = jax.nn.sigmoid((xt @ router_w).astype(jnp.float32))
    _, idx = lax.top_k(scores + router_bias.astype(jnp.float32), TOP_K)
    gates = jnp.take_along_axis(scores, idx, axis=1)
    gates = gates / jnp.sum(gates, axis=-1, keepdims=True) * ROUTED_SCALE
    y = routed_experts(xt, idx, gates, w_gate, w_up, w_down) + swiglu(xt, s_gate, s_up, s_down)
    return y.reshape(n, t, d)


def setup_inputs(seed: int = 0) -> dict:
    key = jax.random.key(seed)
    ks = iter(jax.random.split(key, 64))
    f32 = jnp.float32

    def normal(shape, scale=1.0):
        return jax.random.normal(next(ks), shape, f32) * scale

    def gain(shape):
        return 1.0 + normal(shape, 0.05)

    n_pages = PAST_LEN // PAGE_SIZE
    n_used = DEC_BATCH * n_pages
    n_phys = n_used + n_used // 4
    d = D_MODEL
    ada = 0.3 * d ** -0.5
    return {
        'x_prompt': normal((BATCH, SEQ, d)),
        'x_sample': normal((DEC_BATCH, DEC_SEQ, d)),
        'cache_a_k': normal((n_phys, PAGE_SIZE, N_KV_HEADS, HEAD_DIM)),
        'cache_a_v': normal((n_phys, PAGE_SIZE, N_KV_HEADS, HEAD_DIM)),
        'cache_a_kidx': normal((n_phys, PAGE_SIZE, IDX_DIM)),
        'state_b_k': normal((DEC_BATCH, WINDOW, N_KV_HEADS, HEAD_DIM)),
        'state_b_v': normal((DEC_BATCH, WINDOW, N_KV_HEADS, HEAD_DIM)),
        'state_d_conv': normal((DEC_BATCH, CONV_WIDTH - 1, CONV_DIM), 0.5),
        'page_table': jax.random.permutation(next(ks), n_phys)[:n_used].reshape(DEC_BATCH, n_pages).astype(jnp.int32),
        'c_prompt': normal((BATCH, d)),
        'c_sample': normal((DEC_BATCH, d)),
        'rel_bias': normal((N_BUCKETS, N_HEADS), 0.5),
        'norm_mix_g': gain((DEPTH, d)),
        'ada_mix_w': normal((DEPTH, d, 3 * d), ada),
        'ada_mix_b': normal((DEPTH, 3 * d), 0.02),
        'norm_ffn_g': gain((DEPTH, d)),
        'ada_ffn_w': normal((DEPTH, d, 3 * d), ada),
        'ada_ffn_b': normal((DEPTH, 3 * d), 0.02),
        'a_w_in': normal((d, A_IN_WIDTH), d ** -0.5),
        'a_q_norm': gain((HEAD_DIM,)),
        'a_k_norm': gain((HEAD_DIM,)),
        'a_w_out': normal((Q_WIDTH, d), Q_WIDTH ** -0.5),
        'b_w_in': normal((d, B_IN_WIDTH), d ** -0.5),
        'b_q_norm': gain((HEAD_DIM,)),
        'b_k_norm': gain((HEAD_DIM,)),
        'b_sinks': normal((N_HEADS,), 0.5),
        'b_w_out': normal((Q_WIDTH, d), Q_WIDTH ** -0.5),
        'c_w_in': normal((d, 2 * GMLP_INNER), d ** -0.5),
        'c_ln_g': gain((GMLP_INNER,)),
        'c_ln_b': normal((GMLP_INNER,), 0.02),
        'c_w_spatial': normal((GMLP_GROUPS, CHUNK, CHUNK), CHUNK ** -0.5),
        'c_b_spatial': 1.0 + normal((GMLP_GROUPS, CHUNK), 0.05),
        'c_w_out': normal((GMLP_INNER, d), GMLP_INNER ** -0.5),
        'd_w_in': normal((d, 2 * CONV_DIM), d ** -0.5),
        'd_w_dw': normal((CONV_WIDTH, CONV_DIM), CONV_WIDTH ** -0.5),
        'd_b_dw': normal((CONV_DIM,), 0.02),
        'd_ln_g': gain((CONV_DIM,)),
        'd_ln_b': normal((CONV_DIM,), 0.02),
        'd_w_out': normal((CONV_DIM, d), CONV_DIM ** -0.5),
        'moe_router_w': normal((DEPTH, d, N_EXPERTS), d ** -0.5),
        'moe_router_bias': normal((DEPTH, N_EXPERTS), 0.01),
        'moe_w_gate': normal((DEPTH, N_EXPERTS, d, D_EXPERT), d ** -0.5),
        'moe_w_up': normal((DEPTH, N_EXPERTS, d, D_EXPERT), d ** -0.5),
        'moe_w_down': normal((DEPTH, N_EXPERTS, D_EXPERT, d), D_EXPERT ** -0.5),
        'moe_shared_gate': normal((DEPTH, d, D_EXPERT), d ** -0.5),
        'moe_shared_up': normal((DEPTH, d, D_EXPERT), d ** -0.5),
        'moe_shared_down': normal((DEPTH, D_EXPERT, d), D_EXPERT ** -0.5),
    }


def reference(x_prompt, x_sample, cache_a_k, cache_a_v, cache_a_kidx, state_b_k, state_b_v, state_d_conv,
              page_table, c_prompt, c_sample, rel_bias, norm_mix_g, ada_mix_w, ada_mix_b, norm_ffn_g,
              ada_ffn_w, ada_ffn_b, a_w_in, a_q_norm, a_k_norm, a_w_out, b_w_in, b_q_norm, b_k_norm, b_sinks,
              b_w_out, c_w_in, c_ln_g, c_ln_b, c_w_spatial, c_b_spatial, c_w_out, d_w_in, d_w_dw, d_b_dw,
              d_ln_g, d_ln_b, d_w_out, moe_router_w, moe_router_bias, moe_w_gate, moe_w_up, moe_w_down,
              moe_shared_gate, moe_shared_up, moe_shared_down):
    xp, xs = x_prompt, x_sample
    for layer in range(DEPTH):
        kind = layer % N_MIXERS
        hp, gp = adaln(xp, c_prompt, norm_mix_g[layer], ada_mix_w[layer], ada_mix_b[layer])
        hs, gs = adaln(xs, c_sample, norm_mix_g[layer], ada_mix_w[layer], ada_mix_b[layer])
        if kind == 0:
            mp, a_k_prompt, a_v_prompt, a_kidx_prompt = mixer_a_prompt(
                hp, a_w_in, a_q_norm, a_k_norm, a_w_out, rel_bias)
            ms, a_k_sample, a_v_sample, a_kidx_sample = mixer_a_sample(
                hs, cache_a_k, cache_a_v, cache_a_kidx, page_table, a_w_in, a_q_norm, a_k_norm, a_w_out, rel_bias)
        elif kind == 1:
            mp, b_k_prompt, b_v_prompt = mixer_b_prompt(
                hp, b_w_in, b_q_norm, b_k_norm, b_sinks, b_w_out, rel_bias)
            ms, b_k_sample, b_v_sample = mixer_b_sample(
                hs, state_b_k, state_b_v, b_w_in, b_q_norm, b_k_norm, b_sinks, b_w_out, rel_bias)
        elif kind == 2:
            mp, _ = mixer_c(hp, c_w_in, c_ln_g, c_ln_b, c_w_spatial, c_b_spatial, c_w_out, CHUNK)
            ms, c_v_sample = mixer_c(hs, c_w_in, c_ln_g, c_ln_b, c_w_spatial, c_b_spatial, c_w_out, hs.shape[1])
        else:
            zero_hist = jnp.zeros((hp.shape[0], CONV_WIDTH - 1, CONV_DIM), hp.dtype)
            mp, d_conv_prompt = mixer_d(hp, zero_hist, d_w_in, d_w_dw, d_b_dw, d_ln_g, d_ln_b, d_w_out)
            ms, d_conv_sample = mixer_d(hs, state_d_conv, d_w_in, d_w_dw, d_b_dw, d_ln_g, d_ln_b, d_w_out)
        xp = xp + gp * mp
        xs = xs + gs * ms
        hp, gp = adaln(xp, c_prompt, norm_ffn_g[layer], ada_ffn_w[layer], ada_ffn_b[layer])
        hs, gs = adaln(xs, c_sample, norm_ffn_g[layer], ada_ffn_w[layer], ada_ffn_b[layer])
        xp = xp + gp * moe(hp, moe_router_w[layer], moe_router_bias[layer], moe_w_gate[layer], moe_w_up[layer],
                           moe_w_down[layer], moe_shared_gate[layer], moe_shared_up[layer], moe_shared_down[layer])
        xs = xs + gs * moe(hs, moe_router_w[layer], moe_router_bias[layer], moe_w_gate[layer], moe_w_up[layer],
                           moe_w_down[layer], moe_shared_gate[layer], moe_shared_up[layer], moe_shared_down[layer])
    return (xp, xs, a_k_prompt, a_v_prompt, a_kidx_prompt, a_k_sample, a_v_sample, a_kidx_sample,
            b_k_prompt, b_v_prompt, b_k_sample, b_v_sample, c_v_sample, d_conv_prompt, d_conv_sample)
```

```python
import functools
import math

import jax
import jax.numpy as jnp
import numpy as np
from jax import lax
from jax.experimental import pallas as pl
from jax.experimental.pallas import tpu as pltpu

F32 = jnp.float32
BF16 = jnp.bfloat16
I32 = jnp.int32

D = 1024
N_HEADS = 16
HEAD_DIM = 64
N_KV = 4
GROUP = N_HEADS // N_KV
Q_W = N_HEADS * HEAD_DIM
KV_W = N_KV * HEAD_DIM
ATTN_SCALE = HEAD_DIM ** -0.5
N_BUCKETS = 32
MAX_DISTANCE = 128
IDX_HEADS = 8
IDX_DIM = 64
IDX_SCALE = (IDX_HEADS * IDX_DIM) ** -0.5
TOPK_MAX = 256
BLK = 128
GMLP_GROUPS = 8
CONV_W = 31
N_EXPERTS = 128
TOP_K = 8
D_EXPERT = 256
ROUTED_SCALE = 2.5
EPS = 1e-6
INT_MIN = -(2 ** 31)

VMEM_LIMIT_BYTES = 56 * 1024 * 1024
MOE_ROW_TILE = 256


def _cp(*sem):
    return pltpu.CompilerParams(dimension_semantics=sem, vmem_limit_bytes=VMEM_LIMIT_BYTES)


def _sigmoid(x):
    return 1.0 / (1.0 + jnp.exp(-x))


def _silu(x):
    return x * _sigmoid(x)


def _rms_mod(x, g, scale, shift):
    ms = jnp.mean(x * x, axis=-1, keepdims=True)
    return x * lax.rsqrt(ms + EPS) * g * (1.0 + scale) + shift


def _layer_norm(x, g, b):
    mu = jnp.mean(x, axis=-1, keepdims=True)
    xc = x - mu
    var = jnp.mean(xc * xc, axis=-1, keepdims=True)
    return xc * lax.rsqrt(var + EPS) * g + b


def _mod_spec(lm, tm):
    if lm == 1:
        return pl.BlockSpec((None, 1, 3 * D), lambda b, i: (b, 0, 0))
    return pl.BlockSpec((None, tm, 3 * D), lambda b, i: (b, i, 0))


def _const_spec(shape):
    return pl.BlockSpec(shape, lambda *_: (0,) * len(shape))


def _mods_body(c_ref, w_ref, b_ref, o_ref):
    s = _silu(c_ref[...])
    o_ref[...] = jnp.dot(s.astype(BF16), w_ref[...].astype(BF16), preferred_element_type=F32) + b_ref[...]


def _mods(c_all, w_all, b_all):
    k, n = w_all.shape[0], c_all.shape[0]
    return pl.pallas_call(
        _mods_body,
        grid=(k, 3),
        in_specs=[
            pl.BlockSpec((n, D), lambda l, j: (0, 0)),
            pl.BlockSpec((None, D, D), lambda l, j: (l, 0, j)),
            pl.BlockSpec((None, 1, D), lambda l, j: (l, 0, j)),
        ],
        out_specs=pl.BlockSpec((None, n, D), lambda l, j: (l, 0, j)),
        out_shape=jax.ShapeDtypeStruct((k, n, 3 * D), F32),
        compiler_params=_cp("parallel", "parallel"),
        name="adaln_mods",
    )(c_all, w_all, b_all.reshape(k, 1, 3 * D))


def _bucket_of(dist):
    max_exact = N_BUCKETS // 2
    d = np.maximum(dist, 0)
    log_ratio = np.log(np.maximum(d, 1).astype(np.float32) / np.float32(max_exact)) / np.float32(
        math.log(MAX_DISTANCE / max_exact))
    large = np.minimum(max_exact + (log_ratio * (N_BUCKETS - max_exact)).astype(np.int32), N_BUCKETS - 1)
    return np.where(d < max_exact, d, large).astype(np.int32)


def _bias_table_body(relb_ref, bk_ref, o_ref):
    h = pl.program_id(0)
    bk = bk_ref[...]
    acc = jnp.zeros(bk.shape, F32)
    for b in range(N_BUCKETS):
        acc = jnp.where(bk == b, relb_ref[b, h], acc)
    o_ref[...] = acc


def _bias_table(rel_bias, buckets):
    r, c = buckets.shape
    return pl.pallas_call(
        _bias_table_body,
        grid=(N_HEADS,),
        in_specs=[pl.BlockSpec(memory_space=pltpu.SMEM), _const_spec((r, c))],
        out_specs=pl.BlockSpec((None, r, c), lambda h: (h, 0, 0)),
        out_shape=jax.ShapeDtypeStruct((N_HEADS, r, c), F32),
        compiler_params=_cp("arbitrary"),
        name="t5_bias_table",
    )(rel_bias, jnp.asarray(buckets))


def _head_norm(a, gain, ones_bd):
    ss = jnp.dot((a * a).astype(BF16), ones_bd, preferred_element_type=F32)
    return a * lax.rsqrt(ss * (1.0 / HEAD_DIM) + EPS) * gain


def _attn_proj_body(has_idx, transposed, x_ref, mod_ref, g_ref, w_ref, wkw_ref, qg_ref, kg_ref, ones_ref, *outs):
    x = x_ref[...]
    mod = mod_ref[...]
    h = _rms_mod(x, g_ref[...], mod[:, D:2 * D], mod[:, :D])
    hb = h.astype(BF16)
    z = jnp.dot(hb, w_ref[...], preferred_element_type=F32)
    ones_bd = ones_ref[...]
    qn = _head_norm(z[:, :Q_W], qg_ref[...], ones_bd) * ATTN_SCALE
    kn = _head_norm(z[:, Q_W:Q_W + KV_W], kg_ref[...], ones_bd[:KV_W, :KV_W])
    v = z[:, Q_W + KV_W:Q_W + 2 * KV_W]
    if has_idx:
        qi = z[:, Q_W + 2 * KV_W:]
        kw = jnp.dot(hb, wkw_ref[...], preferred_element_type=F32)
    if transposed:
        if has_idx:
            qT_ref, kh_ref, vT_ref, k_ref, v_ref, qiT_ref, kib_ref, wiT_ref, kidx_ref = outs
        else:
            qT_ref, kh_ref, vT_ref, k_ref, v_ref = outs
        qT_ref[...] = qn.T.astype(BF16)
        vT_ref[...] = v.T.astype(BF16)
        for g in range(N_KV):
            kh_ref[g] = kn[:, g * HEAD_DIM:(g + 1) * HEAD_DIM].astype(BF16)
        k_ref[...] = kn
        v_ref[...] = v
        if has_idx:
            qiT_ref[...] = qi.T.astype(BF16)
            kib_ref[...] = kw[:, :IDX_DIM].astype(BF16)
            wiT_ref[...] = kw.T[IDX_DIM:IDX_DIM + IDX_HEADS, :]
            kidx_ref[...] = kw[:, :IDX_DIM]
    else:
        if has_idx:
            q_ref, k_ref, v_ref, qi_ref, kw_ref = outs
            qi_ref[...] = qi
            kw_ref[...] = kw
        else:
            q_ref, k_ref, v_ref = outs
        q_ref[...] = qn
        k_ref[...] = kn
        v_ref[...] = v


def _attn_proj(x, mod, g, w_main, w_kw, q_gain, k_gain, ones_bd, *, has_idx, transposed, tm):
    b, l, _ = x.shape
    lm = mod.shape[1]
    nmain = w_main.shape[1]
    grid = (b, l // tm)
    row = lambda w: pl.BlockSpec((None, tm, w), lambda bb, i: (bb, i, 0))
    colT = lambda w: pl.BlockSpec((None, w, tm), lambda bb, i: (bb, 0, i))
    if transposed:
        out_specs = [colT(Q_W), pl.BlockSpec((None, N_KV, tm, HEAD_DIM), lambda bb, i: (bb, 0, i, 0)), colT(KV_W),
                     row(KV_W), row(KV_W)]
        out_shape = [jax.ShapeDtypeStruct((b, Q_W, l), BF16), jax.ShapeDtypeStruct((b, N_KV, l, HEAD_DIM), BF16),
                     jax.ShapeDtypeStruct((b, KV_W, l), BF16), jax.ShapeDtypeStruct((b, l, KV_W), F32),
                     jax.ShapeDtypeStruct((b, l, KV_W), F32)]
        if has_idx:
            out_specs += [colT(IDX_HEADS * IDX_DIM), row(IDX_DIM), colT(IDX_HEADS), row(IDX_DIM)]
            out_shape += [jax.ShapeDtypeStruct((b, IDX_HEADS * IDX_DIM, l), BF16),
                          jax.ShapeDtypeStruct((b, l, IDX_DIM), BF16),
                          jax.ShapeDtypeStruct((b, IDX_HEADS, l), F32), jax.ShapeDtypeStruct((b, l, IDX_DIM), F32)]
    else:
        out_specs = [row(Q_W), row(KV_W), row(KV_W)]
        out_shape = [jax.ShapeDtypeStruct((b, l, Q_W), F32), jax.ShapeDtypeStruct((b, l, KV_W), F32),
                     jax.ShapeDtypeStruct((b, l, KV_W), F32)]
        if has_idx:
            out_specs += [row(IDX_HEADS * IDX_DIM), row(BLK)]
            out_shape += [jax.ShapeDtypeStruct((b, l, IDX_HEADS * IDX_DIM), F32), jax.ShapeDtypeStruct((b, l, BLK), F32)]
    return pl.pallas_call(
        functools.partial(_attn_proj_body, has_idx, transposed),
        grid=grid,
        in_specs=[row(D), _mod_spec(lm, tm), _const_spec((1, D)), _const_spec((D, nmain)), _const_spec((D, BLK)),
                  _const_spec((1, Q_W)), _const_spec((1, KV_W)), _const_spec((Q_W, Q_W))],
        out_specs=out_specs,
        out_shape=out_shape,
        compiler_params=_cp("parallel", "parallel"),
        name="attn_in_proj",
    )(x, mod, g, w_main, w_kw, q_gain, k_gain, ones_bd)


def _key_to_float(key):
    bits = jnp.where(key >= 0, key, key ^ jnp.int32(0x7FFFFFFF))
    return lax.bitcast_convert_type(bits, F32)


def _topk_mask(sc_ref, ktop, axis, pos_io):
    other = 1 - axis
    shape1 = tuple(1 if a == axis else n for a, n in enumerate(sc_ref.shape))
    n_axis = sc_ref.shape[axis]
    kf = float(ktop)

    def count(pred_f):
        return jnp.sum(pred_f, axis=axis, keepdims=True)

    def bis(i, cur):
        trial = cur + lax.shift_left(jnp.int32(1), 31 - i)
        cand = _key_to_float(trial)
        cnt = count(jnp.where(sc_ref[...] >= cand, 1.0, 0.0))
        return jnp.where(cnt >= kf, trial, cur)

    cur = lax.fori_loop(0, 32, bis, jnp.full(shape1, INT_MIN, I32))
    c = _key_to_float(cur)
    sc = sc_ref[...]
    thr = jnp.min(jnp.where(sc >= c, sc, jnp.inf), axis=axis, keepdims=True)
    need = kf - count(jnp.where(sc > thr, 1.0, 0.0))

    nbits = max(1, int(math.ceil(math.log2(n_axis))))

    def tie(i, curp):
        trialp = curp + lax.shift_left(jnp.int32(1), nbits - 1 - i)
        cnt = count(jnp.where(sc_ref[...] == thr, jnp.where(pos_io <= trialp, 1.0, 0.0), 0.0))
        return jnp.where(cnt <= need, trialp, curp)

    pbound = lax.fori_loop(0, nbits, tie, jnp.zeros(shape1, I32))
    del other
    return jnp.where(sc > thr, 1.0, jnp.where(sc == thr, jnp.where(pos_io <= pbound, 1.0, 0.0), 0.0))


def _dsa_attn_body(j0, s_keys, ktop, qT_ref, qiT_ref, wiT_ref, kh_ref, vT_ref, kib_ref, dlt_ref, cfar_ref, o_ref,
                   sc_scr, mb_scr, lg_scr, oT_scr):
    j = j0 + pl.program_id(1)
    s_io = lax.broadcasted_iota(I32, (s_keys, BLK), 0)
    t_io = j * BLK + lax.broadcasted_iota(I32, (s_keys, BLK), 1)
    causal = s_io <= t_io

    kib = kib_ref[...]
    acc = jnp.zeros((s_keys, BLK), F32)
    for h in range(IDX_HEADS):
        d = jnp.dot(kib, qiT_ref[h * IDX_DIM:(h + 1) * IDX_DIM, :], preferred_element_type=F32)
        acc = acc + jnp.maximum(d, 0.0) * wiT_ref[h:h + 1, :]
    sc_scr[...] = jnp.where(causal, acc * IDX_SCALE, -jnp.inf)

    sel = _topk_mask(sc_scr, ktop, 0, s_io)
    keep = jnp.where(t_io < ktop, jnp.where(causal, 1.0, 0.0), sel)
    mb_scr[...] = jnp.where(keep > 0.5, 0.0, -jnp.inf)

    lg_scr[0:BLK, :] = jnp.zeros((BLK, BLK), F32)
    win = pl.multiple_of(j * BLK, BLK)

    def head(h, carry):
        g = h // GROUP
        q = qT_ref[pl.ds(pl.multiple_of(h * HEAD_DIM, HEAD_DIM), HEAD_DIM), :]
        lg = jnp.dot(kh_ref[g], q, preferred_element_type=F32) + (mb_scr[...] + cfar_ref[h])
        lg_scr[pl.ds(BLK, s_keys), :] = lg
        lg_scr[pl.ds(win, 2 * BLK), :] = lg_scr[pl.ds(win, 2 * BLK), :] + dlt_ref[h]
        lg = lg_scr[pl.ds(BLK, s_keys), :]
        m = jnp.max(lg, axis=0, keepdims=True)
        p = jnp.exp(lg - m)
        l = jnp.sum(p, axis=0, keepdims=True)
        vt = vT_ref[pl.ds(pl.multiple_of(g * HEAD_DIM, HEAD_DIM), HEAD_DIM), :]
        o = jnp.dot(vt, p.astype(BF16), preferred_element_type=F32)
        oT_scr[pl.ds(pl.multiple_of(h * HEAD_DIM, HEAD_DIM), HEAD_DIM), :] = o / l
        return carry

    lax.fori_loop(0, N_HEADS, head, 0)
    o_ref[...] = oT_scr[...].T.astype(BF16)


def _dsa_attn_prompt(qT, qiT, wiT, kh, vT, kib, dlt, cfar):
    b, _, l = qT.shape
    nblk = l // BLK
    ktop = min(TOPK_MAX, l // 4)
    per_class = 4
    outs = []
    for j0 in range(0, nblk, per_class):
        nj = min(per_class, nblk - j0)
        s_keys = (j0 + nj) * BLK
        colT = lambda w: pl.BlockSpec((None, w, BLK), lambda bb, jj, j0=j0: (bb, 0, j0 + jj))
        outs.append(pl.pallas_call(
            functools.partial(_dsa_attn_body, j0, s_keys, ktop),
            grid=(b, nj),
            in_specs=[colT(Q_W), colT(IDX_HEADS * IDX_DIM), colT(IDX_HEADS),
                      pl.BlockSpec((None, N_KV, s_keys, HEAD_DIM), lambda bb, jj: (bb, 0, 0, 0)),
                      pl.BlockSpec((None, KV_W, s_keys), lambda bb, jj: (bb, 0, 0)),
                      pl.BlockSpec((None, s_keys, IDX_DIM), lambda bb, jj: (bb, 0, 0)),
                      _const_spec((N_HEADS, 2 * BLK, BLK)),
                      pl.BlockSpec(memory_space=pltpu.SMEM)],
            out_specs=pl.BlockSpec((None, BLK, Q_W), lambda bb, jj: (bb, jj, 0)),
            out_shape=jax.ShapeDtypeStruct((b, nj * BLK, Q_W), BF16),
            scratch_shapes=[pltpu.VMEM((s_keys, BLK), F32), pltpu.VMEM((s_keys, BLK), F32),
                            pltpu.VMEM((s_keys + BLK, BLK), F32), pltpu.VMEM((Q_W, BLK), F32)],
            compiler_params=_cp("parallel", "arbitrary"),
            name=f"dsa_attn_s{s_keys}",
        )(qT, qiT, wiT, kh, vT, kib, dlt, cfar))
    return outs[0] if len(outs) == 1 else jnp.concatenate(outs, axis=1)


def _swa_attn_body(qT_ref, khp_ref, khc_ref, vTp_ref, vTc_ref, tab_ref, sink_ref, o_ref, oT_scr):
    j = pl.program_id(1)
    s_io = lax.broadcasted_iota(I32, (2 * BLK, BLK), 0)
    t_io = lax.broadcasted_iota(I32, (2 * BLK, BLK), 1)
    dist = BLK + t_io - s_io
    prev_ok = jnp.where(j > 0, 0.0, -jnp.inf)
    mb = jnp.where(dist >= 0, jnp.where(dist < BLK, jnp.where(s_io >= BLK, 0.0, prev_ok), -jnp.inf), -jnp.inf)

    def head(h, carry):
        g = h // GROUP
        q = qT_ref[pl.ds(pl.multiple_of(h * HEAD_DIM, HEAD_DIM), HEAD_DIM), :]
        k2 = jnp.concatenate([khp_ref[g], khc_ref[g]], axis=0)
        lg = jnp.dot(k2, q, preferred_element_type=F32) + tab_ref[h] + mb
        sink = sink_ref[h]
        m = jnp.maximum(jnp.max(lg, axis=0, keepdims=True), sink)
        p = jnp.exp(lg - m)
        l = jnp.sum(p, axis=0, keepdims=True) + jnp.exp(sink - m)
        rows = pl.ds(pl.multiple_of(g * HEAD_DIM, HEAD_DIM), HEAD_DIM)
        v2 = jnp.concatenate([vTp_ref[rows, :], vTc_ref[rows, :]], axis=1)
        o = jnp.dot(v2, p.astype(BF16), preferred_element_type=F32)
        oT_scr[pl.ds(pl.multiple_of(h * HEAD_DIM, HEAD_DIM), HEAD_DIM), :] = o / l
        return carry

    lax.fori_loop(0, N_HEADS, head, 0)
    o_ref[...] = oT_scr[...].T.astype(BF16)


def _swa_attn_prompt(qT, kh, vT, tab, sinks):
    b, _, l = qT.shape
    nblk = l // BLK
    prev = lambda jj: jnp.maximum(jj - 1, 0)
    return pl.pallas_call(
        _swa_attn_body,
        grid=(b, nblk),
        in_specs=[pl.BlockSpec((None, Q_W, BLK), lambda bb, jj: (bb, 0, jj)),
                  pl.BlockSpec((None, N_KV, BLK, HEAD_DIM), lambda bb, jj: (bb, 0, prev(jj), 0)),
                  pl.BlockSpec((None, N_KV, BLK, HEAD_DIM), lambda bb, jj: (bb, 0, jj, 0)),
                  pl.BlockSpec((None, KV_W, BLK), lambda bb, jj: (bb, 0, prev(jj))),
                  pl.BlockSpec((None, KV_W, BLK), lambda bb, jj: (bb, 0, jj)),
                  _const_spec((N_HEADS, 2 * BLK, BLK)),
                  pl.BlockSpec(memory_space=pltpu.SMEM)],
        out_specs=pl.BlockSpec((None, BLK, Q_W), lambda bb, jj: (bb, jj, 0)),
        out_shape=jax.ShapeDtypeStruct((b, l, Q_W), BF16),
        scratch_shapes=[pltpu.VMEM((Q_W, BLK), F32)],
        compiler_params=_cp("parallel", "arbitrary"),
        name="swa_attn",
    )(qT, kh, kh, vT, vT, tab, sinks)


def _out_proj_body(a_ref, w_ref, x_ref, mod_ref, o_ref):
    gate = mod_ref[...][:, 2 * D:]
    m = jnp.dot(a_ref[...].astype(BF16), w_ref[...], preferred_element_type=F32)
    o_ref[...] = x_ref[...] + gate * m


def _out_proj(a, w, x, mod, *, tm):
    b, l, _ = x.shape
    lm = mod.shape[1]
    row = pl.BlockSpec((None, tm, D), lambda bb, i: (bb, i, 0))
    return pl.pallas_call(
        _out_proj_body,
        grid=(b, l // tm),
        in_specs=[pl.BlockSpec((None, tm, a.shape[-1]), lambda bb, i: (bb, i, 0)), _const_spec(w.shape), row,
                  _mod_spec(lm, tm)],
        out_specs=row,
        out_shape=jax.ShapeDtypeStruct((b, l, D), F32),
        compiler_params=_cp("parallel", "parallel"),
        name="out_proj_residual",
    )(a, w, x, mod)


def _gmlp_in_body(x_ref, mod_ref, g_ref, w_ref, lng_ref, lnb_ref, u_ref, v_ref):
    mod = mod_ref[...]
    h = _rms_mod(x_ref[...], g_ref[...], mod[:, D:2 * D], mod[:, :D])
    z = jnp.dot(h.astype(BF16), w_ref[...], preferred_element_type=F32)
    z = 0.5 * z * (1.0 + lax.erf(z * (2.0 ** -0.5)))
    u_ref[...] = z[:, :D].astype(u_ref.dtype)
    v_ref[...] = _layer_norm(z[:, D:], lng_ref[...], lnb_ref[...]).astype(v_ref.dtype)


def _gmlp_in(x, mod, g, w, ln_g, ln_b, *, tm, act_dtype):
    b, l, _ = x.shape
    lm = mod.shape[1]
    row = pl.BlockSpec((None, tm, D), lambda bb, i: (bb, i, 0))
    return pl.pallas_call(
        _gmlp_in_body,
        grid=(b, l // tm),
        in_specs=[row, _mod_spec(lm, tm), _const_spec((1, D)), _const_spec((D, 2 * D)), _const_spec((1, D)),
                  _const_spec((1, D))],
        out_specs=[row, row],
        out_shape=[jax.ShapeDtypeStruct((b, l, D), act_dtype), jax.ShapeDtypeStruct((b, l, D), act_dtype)],
        compiler_params=_cp("parallel", "parallel"),
        name="gmlp_in_proj",
    )(x, mod, g, w, ln_g, ln_b)


def _gmlp_gate_body(u_ref, v_ref, ws_ref, bsT_ref, w_ref, x_ref, mod_ref, o_ref):
    r_io = lax.broadcasted_iota(I32, (BLK, BLK), 0)
    c_io = lax.broadcasted_iota(I32, (BLK, BLK), 1)
    tril = r_io >= c_io
    v = v_ref[...]
    u = u_ref[...].astype(F32)
    bsT = bsT_ref[...]
    parts = []
    for g in range(GMLP_GROUPS):
        ws = jnp.where(tril, ws_ref[g], 0.0).astype(BF16)
        mixed = jnp.dot(ws, v[:, g * BLK:(g + 1) * BLK], preferred_element_type=F32) + bsT[:, g:g + 1]
        parts.append((u[:, g * BLK:(g + 1) * BLK] * mixed).astype(BF16))
    y = jnp.concatenate(parts, axis=1)
    gate = mod_ref[...][:, 2 * D:]
    o_ref[...] = x_ref[...] + gate * jnp.dot(y, w_ref[...], preferred_element_type=F32)


def _gmlp_gate_prompt(u, v, ws, bsT, w_out, x, mod):
    b, l, _ = x.shape
    row = pl.BlockSpec((None, BLK, D), lambda bb, i: (bb, i, 0))
    return pl.pallas_call(
        _gmlp_gate_body,
        grid=(b, l // BLK),
        in_specs=[row, row, _const_spec((GMLP_GROUPS, BLK, BLK)), _const_spec((BLK, GMLP_GROUPS)),
                  _const_spec((D, D)), row, _mod_spec(1, BLK)],
        out_specs=row,
        out_shape=jax.ShapeDtypeStruct((b, l, D), F32),
        compiler_params=_cp("parallel", "parallel"),
        name="gmlp_gate_out",
    )(u, v, ws, bsT, w_out, x, mod)


def _gmlp_gate_sample_body(u_ref, v_ref, a_ref, b_ref, w_ref, x_ref, mod_ref, o_ref):
    y = u_ref[...] * (v_ref[...] * a_ref[...] + b_ref[...])
    gate = mod_ref[...][:, 2 * D:]
    o_ref[...] = x_ref[...] + gate * jnp.dot(y.astype(BF16), w_ref[...], preferred_element_type=F32)


def _gmlp_gate_sample(u, v, a_vec, b_vec, w_out, x, mod):
    b, l, _ = x.shape
    row = pl.BlockSpec((None, l, D), lambda bb, i: (bb, 0, 0))
    return pl.pallas_call(
        _gmlp_gate_sample_body,
        grid=(b, 1),
        in_specs=[row, row, _const_spec((1, D)), _const_spec((1, D)), _const_spec((D, D)), row, _mod_spec(l, l)],
        out_specs=row,
        out_shape=jax.ShapeDtypeStruct((b, l, D), F32),
        compiler_params=_cp("parallel", "parallel"),
        name="gmlp_gate_out_sample",
    )(u, v, a_vec, b_vec, w_out, x, mod)


def _glu_in_body(x_ref, mod_ref, g_ref, w_ref, y_ref):
    mod = mod_ref[...]
    h = _rms_mod(x_ref[...], g_ref[...], mod[:, D:2 * D], mod[:, :D])
    z = jnp.dot(h.astype(BF16), w_ref[...], preferred_element_type=F32)
    y_ref[...] = z[:, :D] * _sigmoid(z[:, D:])


def _glu_in(x, mod, g, w, *, tm):
    b, l, _ = x.shape
    lm = mod.shape[1]
    row = pl.BlockSpec((None, tm, D), lambda bb, i: (bb, i, 0))
    return pl.pallas_call(
        _glu_in_body,
        grid=(b, l // tm),
        in_specs=[row, _mod_spec(lm, tm), _const_spec((1, D)), _const_spec((D, 2 * D))],
        out_specs=row,
        out_shape=jax.ShapeDtypeStruct((b, l, D), F32),
        compiler_params=_cp("parallel", "parallel"),
        name="glu_in_proj",
    )(x, mod, g, w)


_CONV_PAD = 32


def _conv_tail(conv, bdw_ref, lng_ref, lnb_ref, w_ref, x_ref, mod_ref, o_ref):
    z = _layer_norm(conv + bdw_ref[...], lng_ref[...], lnb_ref[...])
    z = _silu(z)
    gate = mod_ref[...][:, 2 * D:]
    o_ref[...] = x_ref[...] + gate * jnp.dot(z.astype(BF16), w_ref[...], preferred_element_type=F32)


def _conv_prompt_body(tm, yc_ref, yp_ref, wdw_ref, bdw_ref, lng_ref, lnb_ref, w_ref, x_ref, mod_ref, o_ref, buf):
    i = pl.program_id(1)
    hist = yp_ref[tm - _CONV_PAD:, :]
    buf[0:_CONV_PAD, :] = jnp.where(i > 0, hist, 0.0)
    buf[_CONV_PAD:, :] = yc_ref[...]
    off = _CONV_PAD - (CONV_W - 1)
    conv = jnp.zeros((tm, D), F32)
    for j in range(CONV_W):
        conv = conv + buf[off + j:off + j + tm, :] * wdw_ref[j:j + 1, :]
    _conv_tail(conv, bdw_ref, lng_ref, lnb_ref, w_ref, x_ref, mod_ref, o_ref)


def _conv_prompt(y, w_dw, b_dw, ln_g, ln_b, w_out, x, mod, *, tm):
    b, l, _ = x.shape
    row = pl.BlockSpec((None, tm, D), lambda bb, i: (bb, i, 0))
    vec = _const_spec((1, D))
    return pl.pallas_call(
        functools.partial(_conv_prompt_body, tm),
        grid=(b, l // tm),
        in_specs=[row, pl.BlockSpec((None, tm, D), lambda bb, i: (bb, jnp.maximum(i - 1, 0), 0)),
                  _const_spec((_CONV_PAD, D)), vec, vec, vec, _const_spec((D, D)), row, _mod_spec(1, tm)],
        out_specs=row,
        out_shape=jax.ShapeDtypeStruct((b, l, D), F32),
        scratch_shapes=[pltpu.VMEM((tm + _CONV_PAD, D), F32)],
        compiler_params=_cp("parallel", "arbitrary"),
        name="conv_module",
    )(y, y, w_dw, b_dw, ln_g, ln_b, w_out, x, mod)


def _conv_sample_body(y_ref, hist_ref, wdw_ref, bdw_ref, lng_ref, lnb_ref, w_ref, x_ref, mod_ref, o_ref):
    conv = y_ref[...] * wdw_ref[CONV_W - 1:CONV_W, :]
    for j in range(CONV_W - 1):
        conv = conv + hist_ref[j] * wdw_ref[j:j + 1, :]
    _conv_tail(conv, bdw_ref, lng_ref, lnb_ref, w_ref, x_ref, mod_ref, o_ref)


def _conv_sample(y, hist, w_dw, b_dw, ln_g, ln_b, w_out, x, mod):
    b, l, _ = x.shape
    ts = 16 if l % 16 == 0 else l
    row = pl.BlockSpec((None, ts, D), lambda bb, i: (bb, i, 0))
    vec = _const_spec((1, D))
    return pl.pallas_call(
        _conv_sample_body,
        grid=(b, l // ts),
        in_specs=[row, pl.BlockSpec((CONV_W - 1, ts, D), lambda bb, i: (0, i, 0)), _const_spec((_CONV_PAD, D)), vec,
                  vec, vec, _const_spec((D, D)), row, _mod_spec(l, ts)],
        out_specs=row,
        out_shape=jax.ShapeDtypeStruct((b, l, D), F32),
        compiler_params=_cp("parallel", "parallel"),
        name="conv_module_sample",
    )(y, hist, w_dw, b_dw, ln_g, ln_b, w_out, x, mod)


def _ffn_pre_body(x_ref, mod_ref, g_ref, rw_ref, rb_ref, *refs):
    h_ref, idx_ref, gate_ref = refs[-3:]
    mod = mod_ref[...]
    h = _rms_mod(x_ref[...], g_ref[...], mod[:, D:2 * D], mod[:, :D])
    h_ref[...] = h
    scores = _sigmoid(jnp.dot(h.astype(BF16), rw_ref[...], preferred_element_type=F32))
    tm = scores.shape[0]
    lane = lax.broadcasted_iota(I32, (tm, N_EXPERTS), 1)
    col8 = lax.broadcasted_iota(I32, (tm, TOP_K), 1)
    sel = scores + rb_ref[...]
    idx = jnp.zeros((tm, TOP_K), I32)
    gts = jnp.zeros((tm, TOP_K), F32)
    for k in range(TOP_K):
        m = jnp.max(sel, axis=-1, keepdims=True)
        first = jnp.min(jnp.where(sel == m, lane, N_EXPERTS), axis=-1, keepdims=True)
        hit = lane == first
        gk = jnp.sum(jnp.where(hit, scores, 0.0), axis=-1, keepdims=True)
        idx = jnp.where(col8 == k, first, idx)
        gts = jnp.where(col8 == k, gk, gts)
        sel = jnp.where(hit, -jnp.inf, sel)
    idx_ref[...] = idx
    gate_ref[...] = gts / jnp.sum(gts, axis=-1, keepdims=True) * ROUTED_SCALE


def _ffn_pre(x, mod, g, router_w, router_b, t_all, row0, bufs, *, tm):
    b, l, _ = x.shape
    lm = mod.shape[1]
    per_b = l // tm
    blk0 = row0 // tm
    tok = lambda w: pl.BlockSpec((tm, w), lambda bb, i: (blk0 + bb * per_b + i, 0))
    return pl.pallas_call(
        _ffn_pre_body,
        grid=(b, per_b),
        in_specs=[pl.BlockSpec((None, tm, D), lambda bb, i: (bb, i, 0)), _mod_spec(lm, tm), _const_spec((1, D)),
                  _const_spec((D, N_EXPERTS)), _const_spec((1, N_EXPERTS))] + [pl.BlockSpec(memory_space=pl.ANY)] * 3,
        out_specs=[tok(D), tok(TOP_K), tok(TOP_K)],
        out_shape=[jax.ShapeDtypeStruct((t_all, D), F32), jax.ShapeDtypeStruct((t_all, TOP_K), I32),
                   jax.ShapeDtypeStruct((t_all, TOP_K), F32)],
        input_output_aliases={5: 0, 6: 1, 7: 2},
        compiler_params=_cp("parallel", "parallel"),
        name="ffn_adaln_router",
    )(x, mod, g, router_w, router_b, *bufs)


def _dispatch_body(n_tok, dest_ref, h_ref, xs_in_ref, xs_ref, sem):
    del xs_in_ref

    def tok(t, carry):
        for k in range(TOP_K):
            pltpu.make_async_copy(h_ref.at[pl.ds(t, 1), :], xs_ref.at[pl.ds(dest_ref[0, t * TOP_K + k], 1), :],
                                  sem).start()
        return carry

    lax.fori_loop(0, n_tok, tok, 0)

    def wait(t, carry):
        for k in range(TOP_K):
            pltpu.make_async_copy(h_ref.at[pl.ds(t, 1), :], xs_ref.at[pl.ds(0, 1), :], sem).wait()
        return carry

    lax.fori_loop(0, n_tok, wait, 0)


def _dispatch(h_all, dest, xs_buf, *, tm):
    t_all = h_all.shape[0]
    nt = t_all // tm
    return pl.pallas_call(
        functools.partial(_dispatch_body, tm),
        grid=(nt,),
        in_specs=[pl.BlockSpec((None, 1, tm * TOP_K), lambda i: (i, 0, 0), memory_space=pltpu.SMEM),
                  pl.BlockSpec((tm, D), lambda i: (i, 0)), pl.BlockSpec(memory_space=pl.ANY)],
        out_specs=pl.BlockSpec(memory_space=pl.ANY),
        out_shape=jax.ShapeDtypeStruct(xs_buf.shape, F32),
        scratch_shapes=[pltpu.SemaphoreType.DMA(())],
        input_output_aliases={2: 0},
        compiler_params=_cp("arbitrary"),
        name="moe_dispatch",
    )(dest.reshape(nt, 1, tm * TOP_K), h_all, xs_buf)


def _experts_body(be_ref, nu_ref, x_ref, wg_ref, wu_ref, wd_ref, y_ref):
    i = pl.program_id(0)

    @pl.when(i < nu_ref[0])
    def _():
        x = x_ref[...].astype(BF16)
        a = jnp.dot(x, wg_ref[...].astype(BF16), preferred_element_type=F32)
        u = jnp.dot(x, wu_ref[...].astype(BF16), preferred_element_type=F32)
        hmid = (_silu(a) * u).astype(BF16)
        y_ref[...] = jnp.dot(hmid, wd_ref[...].astype(BF16), preferred_element_type=F32)


def _experts(xs, block_e, n_used, w_gate, w_up, w_down):
    n_rows = xs.shape[0]
    nb = n_rows // MOE_ROW_TILE
    rows = pl.BlockSpec((MOE_ROW_TILE, D), lambda i, be, nu: (jnp.minimum(i, nu[0] - 1), 0))
    grid_spec = pltpu.PrefetchScalarGridSpec(
        num_scalar_prefetch=2,
        grid=(nb,),
        in_specs=[rows,
                  pl.BlockSpec((None, D, D_EXPERT), lambda i, be, nu: (be[i], 0, 0)),
                  pl.BlockSpec((None, D, D_EXPERT), lambda i, be, nu: (be[i], 0, 0)),
                  pl.BlockSpec((None, D_EXPERT, D), lambda i, be, nu: (be[i], 0, 0))],
        out_specs=rows,
    )
    return pl.pallas_call(
        _experts_body,
        grid_spec=grid_spec,
        out_shape=jax.ShapeDtypeStruct((n_rows, D), F32),
        compiler_params=_cp("arbitrary"),
        name="moe_experts",
    )(block_e, n_used, xs, w_gate, w_up, w_down)


def _combine_body(n_tok, dest_ref, ys_ref, h_ref, gate_ref, sg_ref, su_ref, sd_ref, x_ref, mod_ref, o_ref, buf, sem):
    def tok(t, carry):
        for k in range(TOP_K):
            pltpu.make_async_copy(ys_ref.at[pl.ds(dest_ref[0, t * TOP_K + k], 1), :], buf.at[k, pl.ds(t, 1), :],
                                  sem).start()
        return carry

    lax.fori_loop(0, n_tok, tok, 0)
    hb = h_ref[...].astype(BF16)
    a = jnp.dot(hb, sg_ref[...], preferred_element_type=F32)
    u = jnp.dot(hb, su_ref[...], preferred_element_type=F32)
    acc = jnp.dot((_silu(a) * u).astype(BF16), sd_ref[...], preferred_element_type=F32)

    def wait(t, carry):
        for k in range(TOP_K):
            pltpu.make_async_copy(ys_ref.at[pl.ds(0, 1), :], buf.at[k, pl.ds(t, 1), :], sem).wait()
        return carry

    lax.fori_loop(0, n_tok, wait, 0)
    gates = gate_ref[...]
    for k in range(TOP_K):
        acc = acc + gates[:, k:k + 1] * buf[k]
    o_ref[...] = x_ref[...] + mod_ref[...][:, 2 * D:] * acc


def _combine(ys, dest, h_all, gates, s_gate, s_up, s_down, x, mod, row0, *, tm):
    b, l, _ = x.shape
    lm = mod.shape[1]
    per_b = l // tm
    blk0 = row0 // tm
    n_tok = b * l
    tok = lambda w: pl.BlockSpec((tm, w), lambda bb, i: (blk0 + bb * per_b + i, 0))
    row = pl.BlockSpec((None, tm, D), lambda bb, i: (bb, i, 0))
    dest_g = lax.slice_in_dim(dest, row0, row0 + n_tok, axis=0).reshape(n_tok // tm, 1, tm * TOP_K)
    return pl.pallas_call(
        functools.partial(_combine_body, tm),
        grid=(b, per_b),
        in_specs=[pl.BlockSpec((None, 1, tm * TOP_K), lambda bb, i: (bb * per_b + i, 0, 0), memory_space=pltpu.SMEM),
                  pl.BlockSpec(memory_space=pl.ANY), tok(D), tok(TOP_K), _const_spec((D, D_EXPERT)),
                  _const_spec((D, D_EXPERT)), _const_spec((D_EXPERT, D)), row, _mod_spec(lm, tm)],
        out_specs=row,
        out_shape=jax.ShapeDtypeStruct((b, l, D), F32),
        scratch_shapes=[pltpu.VMEM((TOP_K, tm, D), F32), pltpu.SemaphoreType.DMA(())],
        compiler_params=_cp("arbitrary", "arbitrary"),
        name="moe_combine",
    )(dest_g, ys, h_all, gates, s_gate, s_up, s_down, x, mod)


def _moe_tile(t_all):
    for tm in (512, 384, 256, 128, 64, 32, 16, 8):
        if t_all % tm == 0:
            return tm
    raise ValueError(f"token count {t_all} must be a multiple of 8")


def _routing_plan(idx, n_rows):
    t_all = idx.shape[0]
    onehot = (idx[:, :, None] == jnp.arange(N_EXPERTS, dtype=I32)[None, None, :]).astype(I32).sum(axis=1)
    before = jnp.cumsum(onehot, axis=0) - onehot
    counts = onehot.sum(axis=0)
    padded = (counts + MOE_ROW_TILE - 1) // MOE_ROW_TILE * MOE_ROW_TILE
    pend = jnp.cumsum(padded)
    pstart = pend - padded
    dest = (pstart[None, :] + before)
    dest = jnp.take_along_axis(dest, idx, axis=1).astype(I32)
    nb = n_rows // MOE_ROW_TILE
    block_e = jnp.minimum(jnp.searchsorted(pend, jnp.arange(nb, dtype=I32) * MOE_ROW_TILE, side="right"),
                          N_EXPERTS - 1).astype(I32)
    n_used = (pend[-1] // MOE_ROW_TILE).astype(I32).reshape(1)
    del t_all
    return dest, block_e, n_used


def _moe_rows(t_all):
    n_assign = t_all * TOP_K
    return (n_assign + N_EXPERTS * (MOE_ROW_TILE - 1) + MOE_ROW_TILE - 1) // MOE_ROW_TILE * MOE_ROW_TILE


def _kidx_scores_body(n_pages, pt_ref, qi_ref, kn_ref, wi_ref, cache_ref, o_ref, kall, sem):
    b = pl.program_id(0)
    nb = pl.num_programs(0)
    past = n_pages * BLK

    def page_copy(bb, slot, p):
        return pltpu.make_async_copy(cache_ref.at[pt_ref[bb, p]],
                                     kall.at[slot, :, pl.ds(pl.multiple_of(p * BLK, BLK), BLK)], sem.at[slot])

    def start(bb, slot):
        def one(p, carry):
            page_copy(bb, slot, p).start()
            return carry
        lax.fori_loop(0, n_pages, one, 0)

    slot = lax.rem(b, 2)

    @pl.when(b == 0)
    def _():
        start(0, 0)

    @pl.when(b + 1 < nb)
    def _():
        start(b + 1, 1 - slot)

    def wait_one(p, carry):
        page_copy(b, slot, p).wait()
        return carry

    lax.fori_loop(0, n_pages, wait_one, 0)

    kall[slot, :, pl.ds(past, BLK)] = jnp.broadcast_to(kn_ref[...], (IDX_DIM, BLK))
    keys = kall[slot].astype(BF16)
    d = jnp.dot(qi_ref[...].astype(BF16), keys, preferred_element_type=F32)
    sc = jnp.sum(jnp.maximum(d, 0.0) * wi_ref[...], axis=0, keepdims=True) * IDX_SCALE
    pos = lax.broadcasted_iota(I32, sc.shape, 1)
    o_ref[...] = jnp.where(pos <= past, sc, -jnp.inf)


def _kidx_scores(page_table, qi, k_new, wi, cache_kidx_t):
    n, n_pages = page_table.shape
    width = n_pages * BLK + BLK
    grid_spec = pltpu.PrefetchScalarGridSpec(
        num_scalar_prefetch=1,
        grid=(n,),
        in_specs=[pl.BlockSpec((None, IDX_HEADS, IDX_DIM), lambda b, pt: (b, 0, 0)),
                  pl.BlockSpec((None, IDX_DIM, 1), lambda b, pt: (b, 0, 0)),
                  pl.BlockSpec((None, IDX_HEADS, 1), lambda b, pt: (b, 0, 0)),
                  pl.BlockSpec(memory_space=pl.ANY)],
        out_specs=pl.BlockSpec((None, 1, width), lambda b, pt: (b, 0, 0)),
        scratch_shapes=[pltpu.VMEM((2, IDX_DIM, width), F32), pltpu.SemaphoreType.DMA((2,))],
    )
    return pl.pallas_call(
        functools.partial(_kidx_scores_body, n_pages),
        grid_spec=grid_spec,
        out_shape=jax.ShapeDtypeStruct((n, 1, width), F32),
        compiler_params=_cp("arbitrary"),
        name="sample_index_scores",
    )(page_table, qi.reshape(n, IDX_HEADS, IDX_DIM), k_new.reshape(n, IDX_DIM, 1), wi.reshape(n, IDX_HEADS, 1),
      cache_kidx_t).reshape(n, width)


def _sample_mask_body(ktop, sc_ref, mb_ref):
    n, width = sc_ref.shape
    pos_io = lax.broadcasted_iota(I32, (n, width), 1)
    sel = _topk_mask(sc_ref, ktop, 1, pos_io)
    mb_ref[...] = jnp.where(sel > 0.5, 0.0, -jnp.inf)


def _sample_mask(scores, ktop):
    n, width = scores.shape
    return pl.pallas_call(
        functools.partial(_sample_mask_body, ktop),
        grid=(1,),
        in_specs=[_const_spec((n, width))],
        out_specs=_const_spec((n, width)),
        out_shape=jax.ShapeDtypeStruct((n, width), F32),
        compiler_params=_cp("arbitrary"),
        name="sample_topk_mask",
    )(scores)


def _q_block_diag(q):
    qt = jnp.concatenate([q] * N_KV, axis=1)
    return jnp.where(_bd_mask(), qt, 0.0)


def _bd_mask():
    r = lax.broadcasted_iota(I32, (N_HEADS, KV_W), 0)
    c = lax.broadcasted_iota(I32, (N_HEADS, KV_W), 1)
    return (r // GROUP) == (c // HEAD_DIM)


def _diag_blocks(acc):
    a = jnp.where(_bd_mask(), acc, 0.0)
    out = a[:, :HEAD_DIM]
    for g in range(1, N_KV):
        out = out + a[:, g * HEAD_DIM:(g + 1) * HEAD_DIM]
    return out


def _paged_attn_body(pps, nh, pt_ref, q_ref, mb_ref, mbn_ref, bias_ref, bnew_ref, kn_ref, vn_ref, ck_ref, cv_ref, o_ref,
                     kbuf, vbuf, m_scr, l_scr, acc_scr, sem):
    b = pl.program_id(0)
    hf = pl.program_id(1)
    step = b * nh + hf
    nsteps = pl.num_programs(0) * nh
    slot = lax.rem(step, 2)

    def copies(st, sl, p):
        bb = st // nh
        page = pt_ref[bb, (st - bb * nh) * pps + p]
        dst = pl.ds(pl.multiple_of(p * BLK, BLK), BLK)
        return (pltpu.make_async_copy(ck_ref.at[page], kbuf.at[sl, :, dst], sem.at[sl]),
                pltpu.make_async_copy(cv_ref.at[page], vbuf.at[sl, :, dst], sem.at[sl]))

    def start(st, sl):
        def one(p, carry):
            ck, cv = copies(st, sl, p)
            ck.start()
            cv.start()
            return carry
        lax.fori_loop(0, pps, one, 0)

    @pl.when(step == 0)
    def _():
        start(0, 0)

    @pl.when(step + 1 < nsteps)
    def _():
        start(step + 1, 1 - slot)

    def wait_one(p, carry):
        ck, cv = copies(step, slot, p)
        ck.wait()
        cv.wait()
        return carry

    lax.fori_loop(0, pps, wait_one, 0)

    @pl.when(hf == 0)
    def _():
        m_scr[...] = jnp.full(m_scr.shape, -jnp.inf, F32)
        l_scr[...] = jnp.zeros(l_scr.shape, F32)
        acc_scr[...] = jnp.zeros(acc_scr.shape, F32)

    qbd = _q_block_diag(q_ref[...]).astype(BF16)

    def update(lg, pv_fn):
        m_prev = m_scr[...]
        m_new = jnp.maximum(m_prev, jnp.max(lg, axis=1, keepdims=True))
        m_safe = jnp.where(m_new == -jnp.inf, 0.0, m_new)
        alpha = jnp.exp(m_prev - m_safe)
        p = jnp.exp(lg - m_safe)
        l_scr[...] = alpha * l_scr[...] + jnp.sum(p, axis=1, keepdims=True)
        acc_scr[...] = alpha * acc_scr[...] + pv_fn(p)
        m_scr[...] = m_new

    lg = jnp.dot(qbd, kbuf[slot].astype(BF16), preferred_element_type=F32) + bias_ref[...] + mb_ref[...]
    update(lg, lambda p: lax.dot_general(p.astype(BF16), vbuf[slot].astype(BF16), (((1,), (1,)), ((), ())),
                                         preferred_element_type=F32))

    @pl.when(hf == nh - 1)
    def _():
        kn = kn_ref[...].astype(BF16).astype(F32)
        vn = vn_ref[...].astype(BF16).astype(F32)
        lg_new = jnp.sum(qbd.astype(F32) * kn, axis=1, keepdims=True) + bnew_ref[...] + mbn_ref[...][:, 0:1]
        update(lg_new, lambda p: p.astype(BF16).astype(F32) * vn)
        o_ref[...] = _diag_blocks(acc_scr[...] / l_scr[...])


def _paged_attn(page_table, q, mb_past, mb_new, bias_steps, bias_new, k_new, v_new, cache_kt, cache_vt):
    n, n_pages = page_table.shape
    nh, _, width = bias_steps.shape
    pps = width // BLK
    assert nh * pps == n_pages
    per_seq = lambda shape: pl.BlockSpec((None,) + shape, lambda b, h, pt: (b, 0, 0))
    grid_spec = pltpu.PrefetchScalarGridSpec(
        num_scalar_prefetch=1,
        grid=(n, nh),
        in_specs=[per_seq((N_HEADS, HEAD_DIM)),
                  pl.BlockSpec((None, 1, width), lambda b, h, pt: (b, 0, h)),
                  per_seq((1, BLK)),
                  pl.BlockSpec((None, N_HEADS, width), lambda b, h, pt: (h, 0, 0)),
                  pl.BlockSpec((N_HEADS, 1), lambda b, h, pt: (0, 0)),
                  per_seq((1, KV_W)), per_seq((1, KV_W)),
                  pl.BlockSpec(memory_space=pl.ANY), pl.BlockSpec(memory_space=pl.ANY)],
        out_specs=per_seq((N_HEADS, HEAD_DIM)),
        scratch_shapes=[pltpu.VMEM((2, KV_W, width), F32), pltpu.VMEM((2, KV_W, width), F32),
                        pltpu.VMEM((N_HEADS, 1), F32), pltpu.VMEM((N_HEADS, 1), F32), pltpu.VMEM((N_HEADS, KV_W), F32),
                        pltpu.SemaphoreType.DMA((2,))],
    )
    return pl.pallas_call(
        functools.partial(_paged_attn_body, pps, nh),
        grid_spec=grid_spec,
        out_shape=jax.ShapeDtypeStruct((n, N_HEADS, HEAD_DIM), F32),
        compiler_params=_cp("arbitrary", "arbitrary"),
        name="sample_paged_attn",
    )(page_table, q, mb_past, mb_new, bias_steps, bias_new, k_new, v_new, cache_kt, cache_vt)


_SWA_SEQS = 8


def _swa_sample_body(q_ref, kk_ref, vv_ref, bias_ref, sink_ref, o_ref):
    bias = bias_ref[...]
    sink = sink_ref[...]
    for s in range(_SWA_SEQS):
        qbd = _q_block_diag(q_ref[s]).astype(BF16)
        lg = jnp.dot(qbd, kk_ref[s].astype(BF16), preferred_element_type=F32) + bias
        m = jnp.maximum(jnp.max(lg, axis=1, keepdims=True), sink)
        p = jnp.exp(lg - m)
        l = jnp.sum(p, axis=1, keepdims=True) + jnp.exp(sink - m)
        acc = lax.dot_general(p.astype(BF16), vv_ref[s].astype(BF16), (((1,), (1,)), ((), ())),
                              preferred_element_type=F32)
        o_ref[s] = _diag_blocks(acc / l)


def _swa_sample(q, kkt, vvt, bias, sinks):
    n = q.shape[0]
    ns = _SWA_SEQS if n % _SWA_SEQS == 0 else n
    assert ns == _SWA_SEQS
    return pl.pallas_call(
        _swa_sample_body,
        grid=(n // ns,),
        in_specs=[pl.BlockSpec((ns, N_HEADS, HEAD_DIM), lambda i: (i, 0, 0)),
                  pl.BlockSpec((ns, KV_W, BLK), lambda i: (i, 0, 0)),
                  pl.BlockSpec((ns, KV_W, BLK), lambda i: (i, 0, 0)),
                  _const_spec((N_HEADS, BLK)), _const_spec((N_HEADS, 1))],
        out_specs=pl.BlockSpec((ns, N_HEADS, HEAD_DIM), lambda i: (i, 0, 0)),
        out_shape=jax.ShapeDtypeStruct((n, N_HEADS, HEAD_DIM), F32),
        compiler_params=_cp("parallel"),
        name="swa_attn_sample",
    )(q, kkt, vvt, bias, sinks)


def _moe_layer(xp, xs, modp, mods, g, router_w, router_b, w_gate, w_up, w_down, s_gate, s_up, s_down, xs_buf, bufs):
    b, l, _ = xp.shape
    n = xs.shape[1]
    t_all = b * l + n
    tmp = 512 if l % 512 == 0 else l
    rw = router_w.astype(BF16)
    rb = router_b.reshape(1, N_EXPERTS)
    bufs = _ffn_pre(xp, modp, g, rw, rb, t_all, 0, bufs, tm=tmp)
    bufs = _ffn_pre(xs, mods, g, rw, rb, t_all, b * l, bufs, tm=n)
    h_all, idx, gates = bufs
    dest, block_e, n_used = _routing_plan(idx, xs_buf.shape[0])
    xs_buf = _dispatch(h_all, dest, xs_buf, tm=_moe_tile(t_all))
    ys = _experts(xs_buf, block_e, n_used, w_gate, w_up, w_down)
    sg, su, sd = s_gate.astype(BF16), s_up.astype(BF16), s_down.astype(BF16)
    xp = _combine(ys, dest, h_all, gates, sg, su, sd, xp, modp, 0, tm=tmp)
    xs = _combine(ys, dest, h_all, gates, sg, su, sd, xs, mods, b * l, tm=n)
    return xp, xs, xs_buf, bufs


def kernel(x_prompt, x_sample, cache_a_k, cache_a_v, cache_a_kidx, state_b_k, state_b_v, state_d_conv, page_table, c_prompt, c_sample, rel_bias, norm_mix_g, ada_mix_w, ada_mix_b, norm_ffn_g, ada_ffn_w, ada_ffn_b, a_w_in, a_q_norm, a_k_norm, a_w_out, b_w_in, b_q_norm, b_k_norm, b_sinks, b_w_out, c_w_in, c_ln_g, c_ln_b, c_w_spatial, c_b_spatial, c_w_out, d_w_in, d_w_dw, d_b_dw, d_ln_g, d_ln_b, d_w_out, moe_router_w, moe_router_bias, moe_w_gate, moe_w_up, moe_w_down, moe_shared_gate, moe_shared_up, moe_shared_down):
    nb, l, _ = x_prompt.shape
    n = x_sample.shape[0]
    depth = norm_mix_g.shape[0]
    assert x_sample.shape[1] == 1 and depth == 4
    n_pages = page_table.shape[1]
    past = n_pages * BLK
    tmp = 512 if l % 512 == 0 else l

    xp = x_prompt
    xs = x_sample.reshape(1, n, D)

    c_all = jnp.concatenate([c_prompt, c_sample], axis=0)
    mods_all = _mods(c_all, jnp.concatenate([ada_mix_w, ada_ffn_w], axis=0),
                     jnp.concatenate([ada_mix_b, ada_ffn_b], axis=0))

    def mod_pair(k):
        return mods_all[k, :nb].reshape(nb, 1, 3 * D), mods_all[k, nb:].reshape(1, n, 3 * D)

    s_i = np.arange(2 * BLK)[:, None]
    t_i = np.arange(BLK)[None, :]
    tab_win = _bias_table(rel_bias, _bucket_of(BLK + t_i - s_i))
    cfar = rel_bias[N_BUCKETS - 1]
    dlt_win = tab_win - cfar[:, None, None]
    tab_s = _bias_table(rel_bias, np.broadcast_to(_bucket_of(BLK - 1 - np.arange(BLK))[None, :], (8, BLK)))[:, 0, :]

    ones_bd = jnp.asarray(np.kron(np.eye(N_HEADS, dtype=np.float32), np.ones((HEAD_DIM, HEAD_DIM), np.float32)), BF16)
    row1 = lambda v: v.reshape(1, -1)
    tile_gain = lambda gvec, reps: jnp.tile(gvec, reps).reshape(1, -1)
    t_all = nb * l + n
    xs_buf = jnp.zeros((_moe_rows(t_all), D), F32)
    moe_bufs = (jnp.zeros((t_all, D), F32), jnp.zeros((t_all, TOP_K), I32), jnp.zeros((t_all, TOP_K), F32))
    outs = {}

    for layer in range(depth):
        modp, mods = mod_pair(layer)
        g_mix = row1(norm_mix_g[layer])
        if layer == 0:
            w_main = a_w_in[:, :Q_W + 2 * KV_W + IDX_HEADS * IDX_DIM].astype(BF16)
            w_kw = jnp.pad(a_w_in[:, Q_W + 2 * KV_W + IDX_HEADS * IDX_DIM:],
                           ((0, 0), (0, BLK - IDX_DIM - IDX_HEADS))).astype(BF16)
            qg, kg = tile_gain(a_q_norm, N_HEADS), tile_gain(a_k_norm, N_KV)
            w_out = a_w_out.astype(BF16)
            qT, kh, vT, k_f, v_f, qiT, kib, wiT, kidx_f = _attn_proj(
                xp, modp, g_mix, w_main, w_kw, qg, kg, ones_bd, has_idx=True, transposed=True, tm=tmp)
            outs["a_k_p"] = k_f.reshape(nb, l, N_KV, HEAD_DIM)
            outs["a_v_p"] = v_f.reshape(nb, l, N_KV, HEAD_DIM)
            outs["a_kidx_p"] = kidx_f
            att = _dsa_attn_prompt(qT, qiT, wiT, kh, vT, kib, dlt_win, cfar)
            xp = _out_proj(att, w_out, xp, modp, tm=tmp)
            q_s, k_s, v_s, qi_s, kw_s = _attn_proj(
                xs, mods, g_mix, w_main, w_kw, qg, kg, ones_bd, has_idx=True, transposed=False, tm=n)
            kidx_s = kw_s[0, :, :IDX_DIM]
            outs["a_k_s"] = k_s.reshape(n, 1, N_KV, HEAD_DIM)
            outs["a_v_s"] = v_s.reshape(n, 1, N_KV, HEAD_DIM)
            outs["a_kidx_s"] = kidx_s.reshape(n, 1, IDX_DIM)
            kidx_t = jnp.transpose(cache_a_kidx, (0, 2, 1))
            cache_kt = jnp.transpose(cache_a_k, (0, 2, 3, 1)).reshape(-1, KV_W, BLK)
            cache_vt = jnp.transpose(cache_a_v, (0, 2, 3, 1)).reshape(-1, KV_W, BLK)
            scores = _kidx_scores(page_table, qi_s[0], kidx_s, kw_s[0, :, IDX_DIM:IDX_DIM + IDX_HEADS], kidx_t)
            ktop = min(TOPK_MAX, (past + 1) // 4)
            mb = _sample_mask(scores, ktop)
            pps = math.gcd(n_pages, 32)
            nh = n_pages // pps
            bias_steps = _bias_table(rel_bias, _bucket_of(past - np.arange(past)).reshape(nh, pps * BLK))
            att_s = _paged_attn(
                page_table, q_s.reshape(n, N_HEADS, HEAD_DIM), mb[:, :past].reshape(n, 1, past),
                mb[:, past:].reshape(n, 1, BLK), jnp.transpose(bias_steps, (1, 0, 2)), rel_bias[0].reshape(N_HEADS, 1),
                k_s.reshape(n, 1, KV_W), v_s.reshape(n, 1, KV_W), cache_kt, cache_vt)
            xs = _out_proj(att_s.reshape(1, n, Q_W), w_out, xs, mods, tm=n)
        elif layer == 1:
            w_main = b_w_in.astype(BF16)
            w_kw = jnp.zeros((D, BLK), BF16)
            qg, kg = tile_gain(b_q_norm, N_HEADS), tile_gain(b_k_norm, N_KV)
            w_out = b_w_out.astype(BF16)
            qT, kh, vT, k_f, v_f = _attn_proj(
                xp, modp, g_mix, w_main, w_kw, qg, kg, ones_bd, has_idx=False, transposed=True, tm=tmp)
            outs["b_k_p"] = k_f[:, l - BLK:].reshape(nb, BLK, N_KV, HEAD_DIM)
            outs["b_v_p"] = v_f[:, l - BLK:].reshape(nb, BLK, N_KV, HEAD_DIM)
            att = _swa_attn_prompt(qT, kh, vT, tab_win, b_sinks)
            xp = _out_proj(att, w_out, xp, modp, tm=tmp)
            q_s, k_s, v_s = _attn_proj(
                xs, mods, g_mix, w_main, w_kw, qg, kg, ones_bd, has_idx=False, transposed=False, tm=n)
            kk = jnp.concatenate([state_b_k[:, 1:], k_s.reshape(n, 1, N_KV, HEAD_DIM)], axis=1)
            vv = jnp.concatenate([state_b_v[:, 1:], v_s.reshape(n, 1, N_KV, HEAD_DIM)], axis=1)
            outs["b_k_s"] = kk
            outs["b_v_s"] = vv
            kkt = jnp.transpose(kk, (0, 2, 3, 1)).reshape(n, KV_W, BLK)
            vvt = jnp.transpose(vv, (0, 2, 3, 1)).reshape(n, KV_W, BLK)
            att_s = _swa_sample(q_s.reshape(n, N_HEADS, HEAD_DIM), kkt, vvt, tab_s, b_sinks.reshape(N_HEADS, 1))
            xs = _out_proj(att_s.reshape(1, n, Q_W), w_out, xs, mods, tm=n)
        elif layer == 2:
            w_in = c_w_in.astype(BF16)
            w_out = c_w_out.astype(BF16)
            u, v = _gmlp_in(xp, modp, g_mix, w_in, row1(c_ln_g), row1(c_ln_b), tm=tmp, act_dtype=BF16)
            xp = _gmlp_gate_prompt(u, v, c_w_spatial, c_b_spatial.T, w_out, xp, modp)
            u_s, v_s = _gmlp_in(xs, mods, g_mix, w_in, row1(c_ln_g), row1(c_ln_b), tm=n, act_dtype=F32)
            outs["c_v_s"] = v_s.reshape(n, 1, D)
            a_vec = jnp.repeat(c_w_spatial[:, 0, 0], D // GMLP_GROUPS).reshape(1, D)
            b_vec = jnp.repeat(c_b_spatial[:, 0], D // GMLP_GROUPS).reshape(1, D)
            xs = _gmlp_gate_sample(u_s, v_s, a_vec, b_vec, w_out, xs, mods)
        else:
            w_in = d_w_in.astype(BF16)
            w_out = d_w_out.astype(BF16)
            w_dw = jnp.pad(d_w_dw, ((0, _CONV_PAD - CONV_W), (0, 0)))
            vecs = (row1(d_b_dw), row1(d_ln_g), row1(d_ln_b))
            y = _glu_in(xp, modp, g_mix, w_in, tm=tmp)
            outs["d_p"] = y[:, l - (CONV_W - 1):]
            xp = _conv_prompt(y, w_dw, *vecs, w_out, xp, modp, tm=min(256, l))
            y_s = _glu_in(xs, mods, g_mix, w_in, tm=n)
            outs["d_s"] = jnp.concatenate([state_d_conv[:, 1:], y_s.reshape(n, 1, D)], axis=1)
            xs = _conv_sample(y_s, jnp.transpose(state_d_conv, (1, 0, 2)), w_dw, *vecs, w_out, xs, mods)

        modp, mods = mod_pair(depth + layer)
        xp, xs, xs_buf, moe_bufs = _moe_layer(
            xp, xs, modp, mods, row1(norm_ffn_g[layer]), moe_router_w[layer], moe_router_bias[layer],
            moe_w_gate[layer], moe_w_up[layer], moe_w_down[layer], moe_shared_gate[layer], moe_shared_up[layer],
            moe_shared_down[layer], xs_buf, moe_bufs)

    return (xp, xs.reshape(n, 1, D), outs["a_k_p"], outs["a_v_p"], outs["a_kidx_p"], outs["a_k_s"], outs["a_v_s"],
            outs["a_kidx_s"], outs["b_k_p"], outs["b_v_p"], outs["b_k_s"], outs["b_v_s"], outs["c_v_s"], outs["d_p"],
            outs["d_s"])
```
